```python
import jax, jax.numpy as jnp
from jax import lax
import numpy as np

D_MODEL = 1024
BATCH = 8
SEQ = 4096
DEPTH = 2
DEC_BATCH = 32
DEC_SEQ = 1
PAST_LEN = 16384
PAGE_SIZE = 128

N_AB = (DEPTH + 1) // 2
N_CD = DEPTH // 2
POOL_WIDTH = D_MODEL // 2
POOL_WINDOWS = (2, 4, 8, 16)
POOL_GROUPS = len(POOL_WINDOWS)
POOL_GROUP_W = POOL_WIDTH // POOL_GROUPS
POOL_BUF = max(POOL_WINDOWS) - 1
SB_WIDTH = D_MODEL // 2
SB_HEAD_DIM = 64
SB_HEADS = SB_WIDTH // SB_HEAD_DIM
SB_BIAS_INIT = -8.0
Q_BLOCK = 128
GM_WIDTH = D_MODEL // 2
GM_GROUPS = 4
GM_GROUP_W = GM_WIDTH // GM_GROUPS
CHUNK = 128
SC_WIDTH = D_MODEL // 2
CONV_W = 3
AB_IN = POOL_WIDTH + 3 * SB_WIDTH
AB_OUT = POOL_WIDTH + SB_WIDTH
CD_IN = 2 * GM_WIDTH + 3 * SC_WIDTH
CD_OUT = GM_WIDTH + SC_WIDTH
D_FF = -(-(8 * D_MODEL) // (3 * 256)) * 256
RMS_EPS = 1e-6

kernel_name = "hybrid_pool_stickbreak_chunkmlp_shortconv_step"


def rmsnorm(x, g):
    xf = x.astype(jnp.float32)
    y = xf * lax.rsqrt(jnp.mean(xf * xf, axis=-1, keepdims=True) + RMS_EPS)
    return (y * g.astype(jnp.float32)).astype(x.dtype)


def swiglu(h, w_gate, w_up, w_down):
    return (jax.nn.silu(h @ w_gate) * (h @ w_up)) @ w_down


def pool_mix(u, prefix, pos0, w_pool, pool_scale):
    B, S, _ = u.shape
    P = prefix.shape[1]
    ext = jnp.concatenate([prefix.astype(u.dtype), u], axis=1)
    cs = jnp.pad(jnp.cumsum(ext.astype(jnp.float32), axis=1), ((0, 0), (1, 0), (0, 0)))
    end = cs[:, P + 1:]
    start = jnp.concatenate(
        [cs[:, P + 1 - w:P + 1 - w + S, g * POOL_GROUP_W:(g + 1) * POOL_GROUP_W]
         for g, w in enumerate(POOL_WINDOWS)], axis=-1)
    pos = pos0 + jnp.arange(S)
    cnt = jnp.minimum(pos[:, None] + 1, jnp.array(POOL_WINDOWS)[None, :]).astype(jnp.float32)
    mean = (end - start).reshape(B, S, POOL_GROUPS, POOL_GROUP_W) / cnt[None, :, :, None]
    d = (mean - u.reshape(B, S, POOL_GROUPS, POOL_GROUP_W).astype(jnp.float32)).astype(u.dtype)
    y = jnp.einsum('bsgc,gcd->bsgd', d, w_pool).reshape(B, S, POOL_WIDTH) * pool_scale
    return y, ext[:, -P:]


def sb_attention(q, k, v, q_pos, k_pos, sb_bias):
    B, Sq, H, Dh = q.shape
    blk = Q_BLOCK if Sq % Q_BLOCK == 0 else Sq
    nb = Sq // blk
    scale = Dh ** -0.5
    qb = q.reshape(B, nb, blk, H, Dh).swapaxes(0, 1)
    pb = q_pos.reshape(nb, blk)
    bias = sb_bias.astype(jnp.float32)[None, :, None, None]

    def block(args):
        qi, pi = args
        z = jnp.einsum('bqhd,bkhd->bhqk', qi, k).astype(jnp.float32) * scale + bias
        mask = k_pos[None, :] < pi[:, None]
        log_keep = jnp.where(mask, jax.nn.log_sigmoid(-z), 0.0)
        later = lax.cumsum(log_keep, axis=3, reverse=True) - log_keep
        w = jnp.where(mask, jnp.exp(jax.nn.log_sigmoid(z) + later), 0.0)
        return jnp.einsum('bhqk,bkhd->bqhd', w.astype(v.dtype), v)

    out = lax.map(block, (qb, pb))
    return out.swapaxes(0, 1).reshape(B, Sq, H, Dh)


def chunk_spatial_gate(u, v, w_s, b_s):
    B, S, _ = v.shape
    pad = (-S) % CHUNK
    nc = (S + pad) // CHUNK
    vp = jnp.pad(v, ((0, 0), (0, pad), (0, 0))).reshape(B, nc, CHUNK, GM_GROUPS, GM_GROUP_W)
    tri = jnp.tril(jnp.ones((CHUNK, CHUNK), dtype=bool))
    wm = jnp.where(tri[None], w_s, jnp.zeros_like(w_s))
    mixed = jnp.einsum('gts,bcsgd->bctgd', wm, vp) + b_s.T[None, None, :, :, None]
    mixed = mixed.reshape(B, nc * CHUNK, GM_WIDTH)[:, :S]
    return u * mixed


def ab_mixer(h, pool_prefix, k_past, v_past, pos0, w_in, sb_bias, w_pool, pool_scale, w_out):
    B, S, _ = h.shape
    proj = h @ w_in
    u, q, k, v = jnp.split(proj, [POOL_WIDTH, POOL_WIDTH + SB_WIDTH, POOL_WIDTH + 2 * SB_WIDTH], axis=-1)
    q = q.reshape(B, S, SB_HEADS, SB_HEAD_DIM)
    k = k.reshape(B, S, SB_HEADS, SB_HEAD_DIM)
    v = v.reshape(B, S, SB_HEADS, SB_HEAD_DIM)
    pool_out, pool_state = pool_mix(u, pool_prefix, pos0, w_pool, pool_scale)
    if k_past is None:
        k_all, v_all = k, v
    else:
        k_all = jnp.concatenate([k_past, k.astype(k_past.dtype)], axis=1)
        v_all = jnp.concatenate([v_past, v.astype(v_past.dtype)], axis=1)
    q_pos = pos0 + jnp.arange(S)
    k_pos = jnp.arange(k_all.shape[1])
    att = sb_attention(q, k_all, v_all, q_pos, k_pos, sb_bias).reshape(B, S, SB_WIDTH)
    y = jnp.concatenate([pool_out.astype(h.dtype), att.astype(h.dtype)], axis=-1) @ w_out
    return y, k, v, pool_state


def cd_mixer(h, conv_prefix, w_in, w_s, b_s, conv_w, w_out):
    B, S, _ = h.shape
    proj = h @ w_in
    uv, hh, bg, cg = jnp.split(proj, [2 * GM_WIDTH, 2 * GM_WIDTH + SC_WIDTH, 2 * GM_WIDTH + 2 * SC_WIDTH], axis=-1)
    u, v = jnp.split(jax.nn.gelu(uv), 2, axis=-1)
    gm = chunk_spatial_gate(u, v, w_s, b_s)
    z = cg * hh
    ext = jnp.concatenate([conv_prefix.astype(z.dtype), z], axis=1)
    conv = sum(ext[:, j:j + S] * conv_w[j] for j in range(CONV_W))
    sc = bg * conv
    y = jnp.concatenate([gm, sc], axis=-1) @ w_out
    return y, ext[:, -(CONV_W - 1):], v


def setup_inputs(seed: int = 0) -> dict:
    key = jax.random.key(seed)
    ks = jax.random.split(key, 24)
    f32 = jnp.float32
    n_pages = PAST_LEN // PAGE_SIZE
    n_used = DEC_BATCH * n_pages
    n_pool = n_used + (n_used + 3) // 4

    def nrm(k, shape, scale):
        return jax.random.normal(k, shape, f32) * scale

    return {
        'x_prompt': nrm(ks[0], (BATCH, SEQ, D_MODEL), 1.0),
        'x_sample': nrm(ks[1], (DEC_BATCH, DEC_SEQ, D_MODEL), 1.0),
        'cache_k': nrm(ks[2], (N_AB, n_pool, PAGE_SIZE, SB_HEADS, SB_HEAD_DIM), 1.0),
        'cache_v': nrm(ks[3], (N_AB, n_pool, PAGE_SIZE, SB_HEADS, SB_HEAD_DIM), 1.0),
        'state_pool': nrm(ks[4], (N_AB, DEC_BATCH, POOL_BUF, POOL_WIDTH), 1.0),
        'state_conv': nrm(ks[5], (N_CD, DEC_BATCH, CONV_W - 1, SC_WIDTH), 1.0),
        'page_table': jax.random.permutation(ks[6], n_pool)[:n_used].reshape(DEC_BATCH, n_pages).astype(jnp.int32),
        'norm_mix': 1.0 + nrm(ks[7], (DEPTH, D_MODEL), 0.05),
        'norm_ffn': 1.0 + nrm(ks[8], (DEPTH, D_MODEL), 0.05),
        'norm_final': 1.0 + nrm(ks[9], (D_MODEL,), 0.05),
        'ab_w_in': nrm(ks[10], (N_AB, D_MODEL, AB_IN), D_MODEL ** -0.5),
        'ab_sb_bias': SB_BIAS_INIT + nrm(ks[22], (N_AB, SB_HEADS), 0.1),
        'ab_w_pool': nrm(ks[11], (N_AB, POOL_GROUPS, POOL_GROUP_W, POOL_GROUP_W), POOL_GROUP_W ** -0.5),
        'ab_pool_scale': 1.0 + nrm(ks[12], (N_AB, POOL_WIDTH), 0.1),
        'ab_w_out': nrm(ks[13], (N_AB, AB_OUT, D_MODEL), AB_OUT ** -0.5),
        'cd_w_in': nrm(ks[14], (N_CD, D_MODEL, CD_IN), D_MODEL ** -0.5),
        'cd_w_s': nrm(ks[15], (N_CD, GM_GROUPS, CHUNK, CHUNK), CHUNK ** -0.5),
        'cd_b_s': 1.0 + nrm(ks[16], (N_CD, GM_GROUPS, CHUNK), 0.1),
        'cd_conv_w': nrm(ks[17], (N_CD, CONV_W, SC_WIDTH), CONV_W ** -0.5),
        'cd_w_out': nrm(ks[18], (N_CD, CD_OUT, D_MODEL), CD_OUT ** -0.5),
        'ffn_w_gate': nrm(ks[19], (DEPTH, D_MODEL, D_FF), D_MODEL ** -0.5),
        'ffn_w_up': nrm(ks[20], (DEPTH, D_MODEL, D_FF), D_MODEL ** -0.5),
        'ffn_w_down': nrm(ks[21], (DEPTH, D_FF, D_MODEL), D_FF ** -0.5),
    }


def reference(x_prompt, x_sample, cache_k, cache_v, state_pool, state_conv, page_table,
              norm_mix, norm_ffn, norm_final,
              ab_w_in, ab_sb_bias, ab_w_pool, ab_pool_scale, ab_w_out,
              cd_w_in, cd_w_s, cd_b_s, cd_conv_w, cd_w_out,
              ffn_w_gate, ffn_w_up, ffn_w_down):
    B = x_prompt.shape[0]
    DB = x_sample.shape[0]
    past = page_table.shape[1] * cache_k.shape[2]
    yp, ys = x_prompt, x_sample
    kp_l, vp_l, ks_l, vs_l = [], [], [], []
    poolp_l, pools_l, convp_l, convs_l, chv_l = [], [], [], [], []
    for l in range(DEPTH):
        i = l // 2
        hp = rmsnorm(yp, norm_mix[l])
        hs = rmsnorm(ys, norm_mix[l])
        if l % 2 == 0:
            mp, kp, vp, poolp = ab_mixer(hp, jnp.zeros((B, POOL_BUF, POOL_WIDTH), hp.dtype), None, None, 0,
                                         ab_w_in[i], ab_sb_bias[i], ab_w_pool[i], ab_pool_scale[i], ab_w_out[i])
            k_past = cache_k[i][page_table].reshape(DB, past, SB_HEADS, SB_HEAD_DIM)
            v_past = cache_v[i][page_table].reshape(DB, past, SB_HEADS, SB_HEAD_DIM)
            ms, ks, vs, pools = ab_mixer(hs, state_pool[i], k_past, v_past, past,
                                         ab_w_in[i], ab_sb_bias[i], ab_w_pool[i], ab_pool_scale[i], ab_w_out[i])
            kp_l.append(kp); vp_l.append(vp); ks_l.append(ks); vs_l.append(vs)
            poolp_l.append(poolp); pools_l.append(pools)
        else:
            mp, convp, _ = cd_mixer(hp, jnp.zeros((B, CONV_W - 1, SC_WIDTH), hp.dtype),
                                    cd_w_in[i], cd_w_s[i], cd_b_s[i], cd_conv_w[i], cd_w_out[i])
            ms, convs, chv = cd_mixer(hs, state_conv[i],
                                      cd_w_in[i], cd_w_s[i], cd_b_s[i], cd_conv_w[i], cd_w_out[i])
            convp_l.append(convp); convs_l.append(convs); chv_l.append(chv)
        yp = yp + mp
        ys = ys + ms
        yp = yp + swiglu(rmsnorm(yp, norm_ffn[l]), ffn_w_gate[l], ffn_w_up[l], ffn_w_down[l])
        ys = ys + swiglu(rmsnorm(ys, norm_ffn[l]), ffn_w_gate[l], ffn_w_up[l], ffn_w_down[l])
    yp = rmsnorm(yp, norm_final)
    ys = rmsnorm(ys, norm_final)
    return (yp, ys, jnp.stack(kp_l), jnp.stack(vp_l), jnp.stack(ks_l), jnp.stack(vs_l),
            jnp.stack(poolp_l), jnp.stack(pools_l), jnp.stack(convp_l), jnp.stack(convs_l), jnp.stack(chv_l))
```

```python
import functools
import math

import jax
import jax.numpy as jnp
from jax import lax
from jax.experimental import pallas as pl
from jax.experimental.pallas import tpu as pltpu

F32 = jnp.float32
BF16 = jnp.bfloat16

V7X_LANES = 128
V7X_SUBLANES = 8
V7X_VMEM_BYTES = 64 * 1024 * 1024
VMEM_LIMIT_BYTES = V7X_VMEM_BYTES - 8 * 1024 * 1024

RMS_EPS = 1e-6
POOL_WINDOWS = (2, 4, 8, 16)
POOL_BUF = max(POOL_WINDOWS) - 1
POOL_CARRY = 16
HEAD_DIM = 64
HEADS_PER_LANE_GROUP = V7X_LANES // HEAD_DIM
CHUNK = 128
CONV_W = 3
CONV_CARRY = V7X_SUBLANES
PAGE = 128

AB_IN_ROWS = 512
ATT_Q_ROWS = 256
ATT_K_ROWS = 256
FFN_ROWS = 512
CD_ROWS = 256
DEC_PAGES_PER_STEP = 8


def _rmsnorm(x, g):
    ms = jnp.mean(x * x, axis=-1, keepdims=True)
    return x * lax.rsqrt(ms + RMS_EPS) * g


def _ffn_residual(y, gffn, wg_ref, wu_ref, wd_ref):
    h = _rmsnorm(y, gffn).astype(BF16)
    g = jnp.dot(h, wg_ref[...], preferred_element_type=F32)
    u = jnp.dot(h, wu_ref[...], preferred_element_type=F32)
    a = (g * jax.nn.sigmoid(g) * u).astype(BF16)
    return y + jnp.dot(a, wd_ref[...], preferred_element_type=F32)


def _const_spec(shape):
    nd = len(shape)
    return pl.BlockSpec(shape, lambda *_: (0,) * nd, pipeline_mode=pl.Buffered(1))


def _params(semantics):
    return pltpu.CompilerParams(dimension_semantics=semantics, vmem_limit_bytes=VMEM_LIMIT_BYTES)


def _pool_mix_groups(win_sums, u, cnt, wpool_ref, pscale):
    gw = wpool_ref.shape[1]
    ys = []
    for g in range(len(POOL_WINDOWS)):
        ug = u[:, g * gw:(g + 1) * gw]
        d = (win_sums[g] / cnt[g] - ug).astype(BF16)
        ys.append(jnp.dot(d, wpool_ref[g], preferred_element_type=F32))
    return jnp.concatenate(ys, axis=1) * pscale


def _ab_in_prompt_kernel(x_ref, g_ref, wuq_ref, wkvt_ref, wpool_ref, pscale_ref,
                         kt_ref, vt_ref, qb_ref, ktb_ref, vtb_ref, po_ref, tail_ref, ext_ref):
    s = pl.program_id(1)
    tm = x_ref.shape[1]
    gw = wpool_ref.shape[1]
    pw = gw * len(POOL_WINDOWS)
    sbw = qb_ref.shape[2]
    h = _rmsnorm(x_ref[0], g_ref[...]).astype(BF16)
    uq = jnp.dot(h, wuq_ref[...], preferred_element_type=F32)
    kvt = lax.dot_general(wkvt_ref[...], h, (((1,), (1,)), ((), ())), preferred_element_type=F32)
    u = uq[:, :pw]
    q = uq[:, pw:]
    kt = kvt[:sbw, :]
    vt = kvt[sbw:, :]
    kt_ref[0] = kt
    vt_ref[0] = vt
    qb_ref[0] = (q * (-(HEAD_DIM ** -0.5))).astype(BF16)
    ktb_ref[0] = kt.astype(BF16)
    vtb_ref[0] = vt.astype(BF16)

    @pl.when(s == 0)
    def _():
        ext_ref[0:POOL_CARRY, :] = jnp.zeros((POOL_CARRY, pw), F32)

    ext_ref[POOL_CARRY:, :] = u
    pos = lax.broadcasted_iota(jnp.int32, (tm, gw), 0) + s * tm
    sums, cnts = [], []
    for g, w in enumerate(POOL_WINDOWS):
        acc = ext_ref[:, g * gw:(g + 1) * gw]
        sh = 1
        while sh < w:
            acc = acc + pltpu.roll(acc, sh, 0)
            sh *= 2
        sums.append(acc[POOL_CARRY:, :])
        cnts.append(jnp.minimum(pos + 1, w).astype(F32))
    po_ref[0] = _pool_mix_groups(sums, u, cnts, wpool_ref, pscale_ref[...]).astype(BF16)
    tail_ref[0] = u[tm - POOL_CARRY:, :]
    ext_ref[0:POOL_CARRY, :] = u[tm - POOL_CARRY:, :]


def _ab_in_prompt(x, g, w_uq, w_kvt, w_pool, pool_scale):
    b, s, d = x.shape
    pw = w_pool.shape[0] * w_pool.shape[1]
    sbw = w_uq.shape[1] - pw
    tm = AB_IN_ROWS
    tile = lambda width: pl.BlockSpec((1, tm, width), lambda i, j: (i, j, 0))
    tile_t = pl.BlockSpec((1, sbw, tm), lambda i, j: (i, 0, j))
    return pl.pallas_call(
        _ab_in_prompt_kernel,
        grid=(b, s // tm),
        in_specs=[tile(d), _const_spec((1, d)), _const_spec(w_uq.shape), _const_spec(w_kvt.shape),
                  _const_spec(w_pool.shape), _const_spec((1, pw))],
        out_specs=[tile_t, tile_t, tile(sbw), tile_t, tile_t, tile(pw),
                   pl.BlockSpec((1, POOL_CARRY, pw), lambda i, j: (i, 0, 0))],
        out_shape=[jax.ShapeDtypeStruct((b, sbw, s), F32), jax.ShapeDtypeStruct((b, sbw, s), F32),
                   jax.ShapeDtypeStruct((b, s, sbw), BF16), jax.ShapeDtypeStruct((b, sbw, s), BF16),
                   jax.ShapeDtypeStruct((b, sbw, s), BF16), jax.ShapeDtypeStruct((b, s, pw), BF16),
                   jax.ShapeDtypeStruct((b, POOL_CARRY, pw), F32)],
        scratch_shapes=[pltpu.VMEM((tm + POOL_CARRY, pw), F32)],
        compiler_params=_params(("parallel", "arbitrary")),
        name="ab_in_prompt",
    )(x, g, w_uq, w_kvt, w_pool, pool_scale)


def _ab_in_sample_kernel(x_ref, g_ref, win_ref, wpool_ref, pscale_ref, st_ref,
                         u_ref, q_ref, k_ref, v_ref, po_ref):
    gw = wpool_ref.shape[1]
    pw = gw * len(POOL_WINDOWS)
    sbw = q_ref.shape[1]
    h = _rmsnorm(x_ref[...], g_ref[...]).astype(BF16)
    proj = jnp.dot(h, win_ref[...], preferred_element_type=F32)
    u = proj[:, :pw]
    u_ref[...] = u
    q_ref[...] = proj[:, pw:pw + sbw] * (-(HEAD_DIM ** -0.5))
    k_ref[...] = proj[:, pw + sbw:pw + 2 * sbw]
    v_ref[...] = proj[:, pw + 2 * sbw:]
    sums, cnts = [], []
    for g, w in enumerate(POOL_WINDOWS):
        acc = u[:, g * gw:(g + 1) * gw]
        for j in range(POOL_BUF - (w - 1), POOL_BUF):
            acc = acc + st_ref[j][:, g * gw:(g + 1) * gw]
        sums.append(acc)
        cnts.append(float(w))
    po_ref[...] = _pool_mix_groups(sums, u, cnts, wpool_ref, pscale_ref[...]).astype(BF16)


def _ab_in_sample(x, g, w_in, w_pool, pool_scale, state_t):
    n, d = x.shape
    pw = w_pool.shape[0] * w_pool.shape[1]
    sbw = (w_in.shape[1] - pw) // 3
    full = lambda a: _const_spec(a.shape)
    out = lambda width, dt: (pl.BlockSpec((n, width), lambda i: (0, 0)), jax.ShapeDtypeStruct((n, width), dt))
    outs = [out(pw, F32), out(sbw, F32), out(sbw, F32), out(sbw, F32), out(pw, BF16)]
    return pl.pallas_call(
        _ab_in_sample_kernel,
        grid=(1,),
        in_specs=[full(x), full(g), full(w_in), full(w_pool), full(pool_scale), full(state_t)],
        out_specs=[o[0] for o in outs],
        out_shape=[o[1] for o in outs],
        compiler_params=_params(("arbitrary",)),
        name="ab_in_sample",
    )(x, g, w_in, w_pool, pool_scale, state_t)


def _log_keep(zn):
    return jnp.minimum(zn, 0.0) - jnp.log(1.0 + jnp.exp(-jnp.abs(zn)))


def _sb_prompt_kernel(nbias_ref, q_ref, k_ref, v_ref, tri_ref, o_ref, acc_ref, r_ref):
    p = pl.program_id(1)
    qi = pl.program_id(2)
    tq = q_ref.shape[1]
    tk = ATT_K_ROWS
    lane = lax.broadcasted_iota(jnp.int32, (tq, V7X_LANES), 1)
    row = lax.broadcasted_iota(jnp.int32, (tq, tk), 0)
    col = lax.broadcasted_iota(jnp.int32, (tq, tk), 1)
    causal = col < row
    qpair = q_ref[0].astype(F32)
    tri = tri_ref[...]

    def tile(qh, nb, ks, mask, r):
        kt = k_ref[0, :, pl.ds(ks, tk)]
        vt = v_ref[0, :, pl.ds(ks, tk)]
        zn = jnp.dot(qh, kt, preferred_element_type=F32) + nb
        lk = _log_keep(zn)
        if mask is not None:
            lk = jnp.where(mask, lk, 0.0)
        later = jnp.dot(lk.astype(BF16), tri, preferred_element_type=F32)
        w = jnp.exp((lk - zn) + (later + jnp.concatenate([r] * (tk // V7X_LANES), axis=1)))
        if mask is not None:
            w = jnp.where(mask, w, 0.0)
        pv = lax.dot_general(w.astype(BF16), vt, (((1,), (1,)), ((), ())), preferred_element_type=F32)
        return pv, jnp.sum(lk, axis=1, keepdims=True)

    outs = []
    for hh in range(HEADS_PER_LANE_GROUP):
        nb = nbias_ref[p * HEADS_PER_LANE_GROUP + hh]
        in_head = (lane >= hh * HEAD_DIM) & (lane < (hh + 1) * HEAD_DIM)
        qh = jnp.where(in_head, qpair, 0.0).astype(BF16)
        pv, rs = tile(qh, nb, pl.multiple_of(qi * tk, tk), causal, jnp.zeros((tq, V7X_LANES), F32))
        acc_ref[...] = pv
        r_ref[...] = jnp.broadcast_to(rs, (tq, V7X_LANES))

        def body(i, carry, qh=qh, nb=nb):
            ks = pl.multiple_of((qi - 1 - i) * tk, tk)
            pv, rs = tile(qh, nb, ks, None, r_ref[...])
            acc_ref[...] += pv
            r_ref[...] += jnp.broadcast_to(rs, (tq, V7X_LANES))
            return carry

        lax.fori_loop(0, qi, body, 0)
        outs.append(acc_ref[...])
    o = outs[0]
    for hh in range(1, HEADS_PER_LANE_GROUP):
        o = jnp.where(lane >= hh * HEAD_DIM, outs[hh], o)
    o_ref[0] = o.astype(BF16)


def _sb_prompt(nbias, qb, ktb, vtb):
    b, s, sbw = qb.shape
    assert ATT_Q_ROWS == ATT_K_ROWS and s % ATT_Q_ROWS == 0
    tq = ATT_Q_ROWS
    n_pairs = sbw // V7X_LANES
    tri = (jnp.arange(ATT_K_ROWS)[:, None] > jnp.arange(ATT_K_ROWS)[None, :]).astype(BF16)
    return pl.pallas_call(
        _sb_prompt_kernel,
        grid=(b, n_pairs, s // tq),
        in_specs=[pl.BlockSpec(memory_space=pltpu.SMEM),
                  pl.BlockSpec((1, tq, V7X_LANES), lambda i, p, j: (i, j, p)),
                  pl.BlockSpec((1, V7X_LANES, s), lambda i, p, j: (i, p, 0)),
                  pl.BlockSpec((1, V7X_LANES, s), lambda i, p, j: (i, p, 0)),
                  _const_spec(tri.shape)],
        out_specs=pl.BlockSpec((1, tq, V7X_LANES), lambda i, p, j: (i, j, p)),
        out_shape=jax.ShapeDtypeStruct((b, s, sbw), BF16),
        scratch_shapes=[pltpu.VMEM((tq, V7X_LANES), F32), pltpu.VMEM((tq, V7X_LANES), F32)],
        compiler_params=_params(("parallel", "parallel", "arbitrary")),
        name="sb_prompt",
    )(nbias, qb, ktb, vtb, tri)


def _sb_decode_kernel(pt_ref, q_ref, nb_ref, tri_ref, *refs):
    del pt_ref
    g_pages = DEC_PAGES_PER_STEP
    k_refs, v_refs = refs[:g_pages], refs[g_pages:2 * g_pages]
    o_ref, acc_ref, r_ref = refs[2 * g_pages:]
    c = pl.program_id(1)
    heads = q_ref.shape[1]

    @pl.when(c == 0)
    def _():
        acc_ref[...] = jnp.zeros(acc_ref.shape, F32)
        r_ref[...] = jnp.zeros(r_ref.shape, F32)

    q = q_ref[0]
    zn = jnp.concatenate([jnp.sum(k_refs[g][0] * q, axis=1) for g in range(g_pages)], axis=0) + nb_ref[...]
    lk = _log_keep(zn)
    within = jnp.dot(lk.astype(BF16), tri_ref[...], preferred_element_type=F32)
    tot = jnp.sum(lk, axis=1, keepdims=True)
    carry = r_ref[...]
    carries = [None] * g_pages
    for g in reversed(range(g_pages)):
        carries[g] = carry
        carry = carry + tot[g * heads:(g + 1) * heads]
    r_ref[...] = carry
    w = jnp.exp((lk - zn) + (within + jnp.concatenate(carries, axis=0)))
    acc = acc_ref[...]
    for g in range(g_pages):
        acc = acc + v_refs[g][0] * w[g * heads:(g + 1) * heads][:, None, :]
    acc_ref[...] = acc

    @pl.when(c == pl.num_programs(1) - 1)
    def _():
        o_ref[0] = jnp.sum(acc, axis=2)


def _sb_decode(page_table, qrep, nbias, cache_kt, cache_vt):
    n, heads = qrep.shape[:2]
    n_pages = page_table.shape[1]
    g_pages = DEC_PAGES_PER_STEP
    assert n_pages % g_pages == 0 and cache_kt.shape[1:] == (heads, HEAD_DIM, PAGE)
    n_steps = n_pages // g_pages
    nb_tile = jnp.tile(jnp.broadcast_to(nbias[:, None], (heads, PAGE)), (g_pages, 1))
    tri = (jnp.arange(PAGE)[:, None] > jnp.arange(PAGE)[None, :]).astype(BF16)

    def page_spec(g):
        return pl.BlockSpec((1, heads, HEAD_DIM, PAGE),
                            lambda i, c, pt: (pt[i, (n_steps - 1 - c) * g_pages + g], 0, 0, 0))

    grid_spec = pltpu.PrefetchScalarGridSpec(
        num_scalar_prefetch=1,
        grid=(n, n_steps),
        in_specs=[pl.BlockSpec((1, heads, HEAD_DIM, PAGE), lambda i, c, pt: (i, 0, 0, 0)),
                  pl.BlockSpec(nb_tile.shape, lambda i, c, pt: (0, 0)),
                  pl.BlockSpec(tri.shape, lambda i, c, pt: (0, 0))]
                 + [page_spec(g) for g in range(g_pages)] + [page_spec(g) for g in range(g_pages)],
        out_specs=pl.BlockSpec((1, heads, HEAD_DIM), lambda i, c, pt: (i, 0, 0)),
        scratch_shapes=[pltpu.VMEM((heads, HEAD_DIM, PAGE), F32), pltpu.VMEM((heads, PAGE), F32)],
    )
    return pl.pallas_call(
        _sb_decode_kernel,
        grid_spec=grid_spec,
        out_shape=jax.ShapeDtypeStruct((n, heads, HEAD_DIM), F32),
        compiler_params=_params(("parallel", "arbitrary")),
        name="sb_decode",
    )(page_table, qrep, nb_tile, tri, *([cache_kt] * g_pages), *([cache_vt] * g_pages))


def _ab_out_ffn_kernel(x_ref, po_ref, att_ref, wout_ref, gffn_ref, wg_ref, wu_ref, wd_ref, y_ref):
    pw = po_ref.shape[1]
    y = (x_ref[...]
         + jnp.dot(po_ref[...], wout_ref[0:pw, :], preferred_element_type=F32)
         + jnp.dot(att_ref[...], wout_ref[pw:, :], preferred_element_type=F32))
    y_ref[...] = _ffn_residual(y, gffn_ref[...], wg_ref, wu_ref, wd_ref)


def _ab_out_ffn(x, po, att, w_out, gffn, wg, wu, wd):
    m, d = x.shape
    tm = min(FFN_ROWS, m)
    tile = lambda width: pl.BlockSpec((tm, width), lambda i: (i, 0))
    return pl.pallas_call(
        _ab_out_ffn_kernel,
        grid=(m // tm,),
        in_specs=[tile(d), tile(po.shape[1]), tile(att.shape[1]), _const_spec(w_out.shape),
                  _const_spec((1, d)), _const_spec(wg.shape), _const_spec(wu.shape), _const_spec(wd.shape)],
        out_specs=tile(d),
        out_shape=jax.ShapeDtypeStruct((m, d), F32),
        compiler_params=_params(("parallel",)),
        name="ab_out_ffn",
    )(x, po, att, w_out, gffn, wg, wu, wd)


def _cd_split(proj, gmw, scw):
    uv = jax.nn.gelu(proj[:, :2 * gmw])
    return (uv[:, :gmw], uv[:, gmw:], proj[:, 2 * gmw:2 * gmw + scw],
            proj[:, 2 * gmw + scw:2 * gmw + 2 * scw], proj[:, 2 * gmw + 2 * scw:])


def _cd_prompt_kernel(x_ref, gmix_ref, win_ref, ws_ref, bs_ref, cw_ref, wout_ref, gffn_ref,
                      wg_ref, wu_ref, wd_ref, gfin_ref, y_ref, ztail_ref, zext_ref):
    s = pl.program_id(1)
    tm = x_ref.shape[1]
    n_groups = ws_ref.shape[0]
    gmw = n_groups * ws_ref.shape[2]
    scw = cw_ref.shape[1]
    x = x_ref[0]
    h = _rmsnorm(x, gmix_ref[...]).astype(BF16)
    proj = jnp.dot(h, win_ref[...], preferred_element_type=F32)
    u, v, hh, bg, cg = _cd_split(proj, gmw, scw)

    r = lax.broadcasted_iota(jnp.int32, (CHUNK, CHUNK), 0)
    c = lax.broadcasted_iota(jnp.int32, (CHUNK, CHUNK), 1)
    vb = v.astype(BF16)
    gwid = gmw // n_groups
    wms = [jnp.where(c <= r, ws_ref[g], 0.0).astype(BF16) for g in range(n_groups)]
    rows = []
    for ch in range(tm // CHUNK):
        cols = [jnp.dot(wms[g], vb[ch * CHUNK:(ch + 1) * CHUNK, g * gwid:(g + 1) * gwid],
                        preferred_element_type=F32) for g in range(n_groups)]
        rows.append(jnp.concatenate(cols, axis=1) + bs_ref[...])
    gm = u * jnp.concatenate(rows, axis=0)

    z = cg * hh

    @pl.when(s == 0)
    def _():
        zext_ref[0:CONV_CARRY, :] = jnp.zeros((CONV_CARRY, scw), F32)

    zext_ref[CONV_CARRY:, :] = z
    zext = zext_ref[...]
    conv = z * cw_ref[CONV_W - 1:CONV_W, :]
    for j in range(CONV_W - 1):
        back = CONV_W - 1 - j
        conv = conv + pltpu.roll(zext, back, 0)[CONV_CARRY:, :] * cw_ref[j:j + 1, :]
    sc = bg * conv
    ztail_ref[0] = z[tm - CONV_CARRY:, :]
    zext_ref[0:CONV_CARRY, :] = z[tm - CONV_CARRY:, :]

    y = (x + jnp.dot(gm.astype(BF16), wout_ref[0:gmw, :], preferred_element_type=F32)
         + jnp.dot(sc.astype(BF16), wout_ref[gmw:, :], preferred_element_type=F32))
    y = _ffn_residual(y, gffn_ref[...], wg_ref, wu_ref, wd_ref)
    y_ref[0] = _rmsnorm(y, gfin_ref[...])


def _cd_prompt(x, gmix, w_in, w_s, bs_exp, conv_w, w_out, gffn, wg, wu, wd, gfin):
    b, s, d = x.shape
    scw = conv_w.shape[1]
    tm = CD_ROWS
    assert tm % CHUNK == 0 and s % tm == 0
    return pl.pallas_call(
        _cd_prompt_kernel,
        grid=(b, s // tm),
        in_specs=[pl.BlockSpec((1, tm, d), lambda i, j: (i, j, 0)), _const_spec((1, d)),
                  _const_spec(w_in.shape), _const_spec(w_s.shape), _const_spec(bs_exp.shape),
                  _const_spec(conv_w.shape), _const_spec(w_out.shape), _const_spec((1, d)),
                  _const_spec(wg.shape), _const_spec(wu.shape), _const_spec(wd.shape), _const_spec((1, d))],
        out_specs=[pl.BlockSpec((1, tm, d), lambda i, j: (i, j, 0)),
                   pl.BlockSpec((1, CONV_CARRY, scw), lambda i, j: (i, 0, 0))],
        out_shape=[jax.ShapeDtypeStruct((b, s, d), F32), jax.ShapeDtypeStruct((b, CONV_CARRY, scw), F32)],
        scratch_shapes=[pltpu.VMEM((tm + CONV_CARRY, scw), F32)],
        compiler_params=_params(("parallel", "arbitrary")),
        name="cd_prompt",
    )(x, gmix, w_in, w_s, bs_exp, conv_w, w_out, gffn, wg, wu, wd, gfin)


def _cd_sample_kernel(x_ref, gmix_ref, win_ref, ws_ref, bs_ref, cw_ref, cst_ref, wout_ref, gffn_ref,
                      wg_ref, wu_ref, wd_ref, gfin_ref, y_ref, z_ref, v_ref):
    n_groups = ws_ref.shape[0]
    gwid = ws_ref.shape[2]
    gmw = n_groups * gwid
    scw = cw_ref.shape[1]
    x = x_ref[...]
    h = _rmsnorm(x, gmix_ref[...]).astype(BF16)
    proj = jnp.dot(h, win_ref[...], preferred_element_type=F32)
    u, v, hh, bg, cg = _cd_split(proj, gmw, scw)
    v_ref[...] = v
    vb = v.astype(BF16).astype(F32)
    mixed = [vb[:, g * gwid:(g + 1) * gwid] * ws_ref[g, 0:1, 0:1].astype(BF16).astype(F32)
             for g in range(n_groups)]
    gm = u * (jnp.concatenate(mixed, axis=1) + bs_ref[0:1, :])
    z = cg * hh
    z_ref[...] = z
    conv = z * cw_ref[CONV_W - 1:CONV_W, :]
    for j in range(CONV_W - 1):
        conv = conv + cst_ref[j] * cw_ref[j:j + 1, :]
    sc = bg * conv
    y = (x + jnp.dot(gm.astype(BF16), wout_ref[0:gmw, :], preferred_element_type=F32)
         + jnp.dot(sc.astype(BF16), wout_ref[gmw:, :], preferred_element_type=F32))
    y = _ffn_residual(y, gffn_ref[...], wg_ref, wu_ref, wd_ref)
    y_ref[...] = _rmsnorm(y, gfin_ref[...])


def _cd_sample(x, gmix, w_in, w_s, bs_exp, conv_w, conv_state_t, w_out, gffn, wg, wu, wd, gfin):
    n, d = x.shape
    scw = conv_w.shape[1]
    gmw = w_s.shape[0] * w_s.shape[2]
    ins = (x, gmix, w_in, w_s, bs_exp, conv_w, conv_state_t, w_out, gffn, wg, wu, wd, gfin)
    full = lambda a: _const_spec(a.shape)
    out = lambda width: (pl.BlockSpec((n, width), lambda i: (0, 0)), jax.ShapeDtypeStruct((n, width), F32))
    outs = [out(d), out(scw), out(gmw)]
    return pl.pallas_call(
        _cd_sample_kernel,
        grid=(1,),
        in_specs=[full(a) for a in ins],
        out_specs=[o[0] for o in outs],
        out_shape=[o[1] for o in outs],
        compiler_params=_params(("arbitrary",)),
        name="cd_sample",
    )(*ins)


def kernel(x_prompt, x_sample, cache_k, cache_v, state_pool, state_conv, page_table, norm_mix, norm_ffn,
           norm_final, ab_w_in, ab_sb_bias, ab_w_pool, ab_pool_scale, ab_w_out, cd_w_in, cd_w_s, cd_b_s,
           cd_conv_w, cd_w_out, ffn_w_gate, ffn_w_up, ffn_w_down):
    b, s, d = x_prompt.shape
    n = x_sample.shape[0]
    heads = ab_sb_bias.shape[1]
    assert x_sample.shape[1] == 1 and norm_mix.shape[0] == 2
    bf = lambda a: a.astype(BF16)
    row = lambda a: a.reshape(1, -1)
    xs = x_sample.reshape(n, d)

    w_in, w_pool, w_out = bf(ab_w_in[0]), bf(ab_w_pool[0]), bf(ab_w_out[0])
    wg, wu, wd = bf(ffn_w_gate[0]), bf(ffn_w_up[0]), bf(ffn_w_down[0])
    gmix, gffn, pscale = row(norm_mix[0]), row(norm_ffn[0]), row(ab_pool_scale[0])
    nbias = -ab_sb_bias[0]
    sbw = heads * HEAD_DIM
    n_uq = w_in.shape[1] - 2 * sbw

    ktp, vtp, qb, ktb, vtb, po, u_tail = _ab_in_prompt(x_prompt, gmix, w_in[:, :n_uq], w_in[:, n_uq:].T,
                                                       w_pool, pscale)
    att = _sb_prompt(nbias, qb, ktb, vtb)
    yp = _ab_out_ffn(x_prompt.reshape(b * s, d), po.reshape(b * s, -1), att.reshape(b * s, -1),
                     w_out, gffn, wg, wu, wd).reshape(b, s, d)

    state_t = jnp.swapaxes(state_pool[0], 0, 1)
    us, qs, ks, vs, pos = _ab_in_sample(xs, gmix, w_in, w_pool, pscale, state_t)
    qrep = jnp.broadcast_to(qs.reshape(n, heads, HEAD_DIM, 1), (n, heads, HEAD_DIM, PAGE))
    cache_kt = jnp.transpose(cache_k[0], (0, 2, 3, 1))
    cache_vt = jnp.transpose(cache_v[0], (0, 2, 3, 1))
    att_s = _sb_decode(page_table, qrep, nbias, cache_kt, cache_vt)
    ys = _ab_out_ffn(xs, pos, bf(att_s.reshape(n, sbw)), w_out, gffn, wg, wu, wd)

    k_prompt = jnp.transpose(ktp.reshape(b, heads, HEAD_DIM, s), (0, 3, 1, 2))[None]
    v_prompt = jnp.transpose(vtp.reshape(b, heads, HEAD_DIM, s), (0, 3, 1, 2))[None]
    k_sample = ks.reshape(1, n, 1, heads, HEAD_DIM)
    v_sample = vs.reshape(1, n, 1, heads, HEAD_DIM)
    pool_prompt = u_tail[None, :, POOL_CARRY - POOL_BUF:, :]
    pool_sample = jnp.concatenate([state_pool[0][:, 1:, :], us[:, None, :]], axis=1)[None]

    w_in, w_out = bf(cd_w_in[0]), bf(cd_w_out[0])
    wg, wu, wd = bf(ffn_w_gate[1]), bf(ffn_w_up[1]), bf(ffn_w_down[1])
    gmix, gffn, gfin = row(norm_mix[1]), row(norm_ffn[1]), row(norm_final)
    gwid = cd_w_s.shape[3]
    bs_exp = jnp.repeat(cd_b_s[0].T, gwid, axis=1)
    conv_w = cd_conv_w[0]

    yp, z_tail = _cd_prompt(yp, gmix, w_in, cd_w_s[0], bs_exp, conv_w, w_out, gffn, wg, wu, wd, gfin)
    conv_state_t = jnp.swapaxes(state_conv[0], 0, 1)
    ys, zs, chv = _cd_sample(ys, gmix, w_in, cd_w_s[0], bs_exp, conv_w, conv_state_t, w_out, gffn,
                             wg, wu, wd, gfin)

    conv_prompt = z_tail[None, :, CONV_CARRY - (CONV_W - 1):, :]
    conv_sample = jnp.concatenate([state_conv[0][:, 1:, :], zs[:, None, :]], axis=1)[None]
    return (yp, ys.reshape(n, 1, d), k_prompt, v_prompt, k_sample, v_sample, pool_prompt, pool_sample,
            conv_prompt, conv_sample, chv.reshape(1, n, 1, -1))
```

```python
import functools
import math

import jax
import jax.numpy as jnp
from jax import lax
from jax.experimental import pallas as pl
from jax.experimental.pallas import tpu as pltpu

F32 = jnp.float32
BF16 = jnp.bfloat16

V7X_LANES = 128
V7X_SUBLANES = 8
V7X_VMEM_BYTES = 64 * 1024 * 1024
VMEM_LIMIT_BYTES = V7X_VMEM_BYTES - 8 * 1024 * 1024

RMS_EPS = 1e-6
POOL_WINDOWS = (2, 4, 8, 16)
POOL_BUF = max(POOL_WINDOWS) - 1
POOL_CARRY = 16
HEAD_DIM = 64
HEADS_PER_LANE_GROUP = V7X_LANES // HEAD_DIM
CHUNK = 128
CONV_W = 3
CONV_CARRY = V7X_SUBLANES
PAGE = 128

AB_IN_ROWS = 512
ATT_Q_ROWS = 256
ATT_K_ROWS = 256
ATT_HEADS_PER_STEP = 8
ATT_MASKED = 1e30
FFN_ROWS = 512
CD_ROWS = 256
DEC_PAGES_PER_STEP = 8


def _rmsnorm(x, g):
    ms = jnp.mean(x * x, axis=-1, keepdims=True)
    return x * lax.rsqrt(ms + RMS_EPS) * g


def _ffn_residual(y, gffn, wg_ref, wu_ref, wd_ref):
    h = _rmsnorm(y, gffn).astype(BF16)
    g = jnp.dot(h, wg_ref[...], preferred_element_type=F32)
    u = jnp.dot(h, wu_ref[...], preferred_element_type=F32)
    a = (g * jax.nn.sigmoid(g) * u).astype(BF16)
    return y + jnp.dot(a, wd_ref[...], preferred_element_type=F32)


def _const_spec(shape):
    nd = len(shape)
    return pl.BlockSpec(shape, lambda *_: (0,) * nd, pipeline_mode=pl.Buffered(1))


def _params(semantics, flags=None):
    return pltpu.CompilerParams(dimension_semantics=semantics, vmem_limit_bytes=VMEM_LIMIT_BYTES, flags=flags)


def _pool_mix_groups(win_sums, u, cnt, wpool_ref, pscale):
    gw = wpool_ref.shape[1]
    ys = []
    for g in range(len(POOL_WINDOWS)):
        ug = u[:, g * gw:(g + 1) * gw]
        d = (win_sums[g] / cnt[g] - ug).astype(BF16)
        ys.append(jnp.dot(d, wpool_ref[g], preferred_element_type=F32))
    return jnp.concatenate(ys, axis=1) * pscale


def _ab_in_prompt_kernel(x_ref, g_ref, wuq_ref, wkvt_ref, wpool_ref, pscale_ref,
                         kt_ref, vt_ref, qb_ref, ktb_ref, vtb_ref, po_ref, tail_ref, ext_ref):
    s = pl.program_id(1)
    tm = x_ref.shape[1]
    gw = wpool_ref.shape[1]
    pw = gw * len(POOL_WINDOWS)
    sbw = qb_ref.shape[2]
    h = _rmsnorm(x_ref[0], g_ref[...]).astype(BF16)
    uq = jnp.dot(h, wuq_ref[...], preferred_element_type=F32)
    kvt = lax.dot_general(wkvt_ref[...], h, (((1,), (1,)), ((), ())), preferred_element_type=F32)
    u = uq[:, :pw]
    q = uq[:, pw:]
    kt = kvt[:sbw, :]
    vt = kvt[sbw:, :]
    kt_ref[0] = kt
    vt_ref[0] = vt
    qb_ref[0] = (q * (-(HEAD_DIM ** -0.5))).astype(BF16)
    ktb_ref[0] = kt.astype(BF16)
    vtb_ref[0] = vt.astype(BF16)

    @pl.when(s == 0)
    def _():
        ext_ref[0:POOL_CARRY, :] = jnp.zeros((POOL_CARRY, pw), F32)

    ext_ref[POOL_CARRY:, :] = u
    pos = lax.broadcasted_iota(jnp.int32, (tm, gw), 0) + s * tm
    sums, cnts = [], []
    for g, w in enumerate(POOL_WINDOWS):
        acc = ext_ref[:, g * gw:(g + 1) * gw]
        sh = 1
        while sh < w:
            acc = acc + pltpu.roll(acc, sh, 0)
            sh *= 2
        sums.append(acc[POOL_CARRY:, :])
        cnts.append(jnp.minimum(pos + 1, w).astype(F32))
    po_ref[0] = _pool_mix_groups(sums, u, cnts, wpool_ref, pscale_ref[...]).astype(BF16)
    tail_ref[0] = u[tm - POOL_CARRY:, :]
    ext_ref[0:POOL_CARRY, :] = u[tm - POOL_CARRY:, :]


def _ab_in_prompt(x, g, w_uq, w_kvt, w_pool, pool_scale):
    b, s, d = x.shape
    pw = w_pool.shape[0] * w_pool.shape[1]
    sbw = w_uq.shape[1] - pw
    tm = AB_IN_ROWS
    tile = lambda width: pl.BlockSpec((1, tm, width), lambda i, j: (i, j, 0))
    tile_t = pl.BlockSpec((1, sbw, tm), lambda i, j: (i, 0, j))
    return pl.pallas_call(
        _ab_in_prompt_kernel,
        grid=(b, s // tm),
        in_specs=[tile(d), _const_spec((1, d)), _const_spec(w_uq.shape), _const_spec(w_kvt.shape),
                  _const_spec(w_pool.shape), _const_spec((1, pw))],
        out_specs=[tile_t, tile_t, tile(sbw), tile_t, tile_t, tile(pw),
                   pl.BlockSpec((1, POOL_CARRY, pw), lambda i, j: (i, 0, 0))],
        out_shape=[jax.ShapeDtypeStruct((b, sbw, s), F32), jax.ShapeDtypeStruct((b, sbw, s), F32),
                   jax.ShapeDtypeStruct((b, s, sbw), BF16), jax.ShapeDtypeStruct((b, sbw, s), BF16),
                   jax.ShapeDtypeStruct((b, sbw, s), BF16), jax.ShapeDtypeStruct((b, s, pw), BF16),
                   jax.ShapeDtypeStruct((b, POOL_CARRY, pw), F32)],
        scratch_shapes=[pltpu.VMEM((tm + POOL_CARRY, pw), F32)],
        compiler_params=_params(("parallel", "arbitrary")),
        name="ab_in_prompt",
    )(x, g, w_uq, w_kvt, w_pool, pool_scale)


def _ab_in_sample_kernel(x_ref, g_ref, win_ref, wpool_ref, pscale_ref, st_ref,
                         u_ref, q_ref, k_ref, v_ref, po_ref):
    gw = wpool_ref.shape[1]
    pw = gw * len(POOL_WINDOWS)
    sbw = q_ref.shape[1]
    h = _rmsnorm(x_ref[...], g_ref[...]).astype(BF16)
    proj = jnp.dot(h, win_ref[...], preferred_element_type=F32)
    u = proj[:, :pw]
    u_ref[...] = u
    q_ref[...] = proj[:, pw:pw + sbw] * (-(HEAD_DIM ** -0.5))
    k_ref[...] = proj[:, pw + sbw:pw + 2 * sbw]
    v_ref[...] = proj[:, pw + 2 * sbw:]
    sums, cnts = [], []
    for g, w in enumerate(POOL_WINDOWS):
        acc = u[:, g * gw:(g + 1) * gw]
        for j in range(POOL_BUF - (w - 1), POOL_BUF):
            acc = acc + st_ref[j][:, g * gw:(g + 1) * gw]
        sums.append(acc)
        cnts.append(float(w))
    po_ref[...] = _pool_mix_groups(sums, u, cnts, wpool_ref, pscale_ref[...]).astype(BF16)


def _ab_in_sample(x, g, w_in, w_pool, pool_scale, state_t):
    n, d = x.shape
    pw = w_pool.shape[0] * w_pool.shape[1]
    sbw = (w_in.shape[1] - pw) // 3
    full = lambda a: _const_spec(a.shape)
    out = lambda width, dt: (pl.BlockSpec((n, width), lambda i: (0, 0)), jax.ShapeDtypeStruct((n, width), dt))
    outs = [out(pw, F32), out(sbw, F32), out(sbw, F32), out(sbw, F32), out(pw, BF16)]
    return pl.pallas_call(
        _ab_in_sample_kernel,
        grid=(1,),
        in_specs=[full(x), full(g), full(w_in), full(w_pool), full(pool_scale), full(state_t)],
        out_specs=[o[0] for o in outs],
        out_shape=[o[1] for o in outs],
        compiler_params=_params(("arbitrary",)),
        name="ab_in_sample",
    )(x, g, w_in, w_pool, pool_scale, state_t)


def _neg_abs(x):
    bits = lax.bitcast_convert_type(x, jnp.uint32) | jnp.uint32(0x80000000)
    return lax.bitcast_convert_type(bits, F32)


def _log_keep(zn):
    return jnp.minimum(zn, 0.0) - jnp.log(1.0 + jnp.exp(_neg_abs(zn)))


def _sb_prompt_kernel(nbias_ref, q_ref, k_ref, v_ref, tri_ref, o_ref, acc_ref, r_ref, z_ref, lw_ref):
    grp = pl.program_id(1)
    qi = pl.program_id(2)
    tq = q_ref.shape[1]
    tk = ATT_K_ROWS
    n_heads = acc_ref.shape[0]
    lane = lax.broadcasted_iota(jnp.int32, (tq, V7X_LANES), 1)
    tri = tri_ref[...]

    qhs, nbs = [], []
    for h in range(n_heads):
        pair, hh = divmod(h, HEADS_PER_LANE_GROUP)
        qpair = q_ref[0, :, pair * V7X_LANES:(pair + 1) * V7X_LANES].astype(F32)
        in_head = (lane >= hh * HEAD_DIM) & (lane < (hh + 1) * HEAD_DIM)
        qhs.append(jnp.where(in_head, qpair, 0.0).astype(BF16))
        nbs.append(nbias_ref[grp * n_heads + h])

    def kv_tile(ref, h, j):
        chans = pl.ds((h // HEADS_PER_LANE_GROUP) * V7X_LANES, V7X_LANES)
        return ref[0, chans, pl.ds(pl.multiple_of(j * tk, tk), tk)]

    def stage_a(h, j, mask=None):
        zn = jnp.dot(qhs[h], kv_tile(k_ref, h, j), preferred_element_type=F32) + nbs[h]
        z_ref[h] = zn if mask is None else jnp.where(mask, zn, ATT_MASKED)

    def stage_b(h):
        zn = z_ref[h]
        lk = _log_keep(zn)
        suffix = jnp.dot(lk.astype(BF16), tri, preferred_element_type=F32)
        r = r_ref[h]
        lw_ref[h] = (lk - zn) + (suffix + jnp.concatenate([r] * (tk // V7X_LANES), axis=1))
        r_ref[h] = r + jnp.broadcast_to(jnp.sum(lk, axis=1, keepdims=True), (tq, V7X_LANES))

    def stage_c(h, j):
        w = jnp.exp(lw_ref[h]).astype(BF16)
        acc_ref[h] += lax.dot_general(w, kv_tile(v_ref, h, j), (((1,), (1,)), ((), ())),
                                      preferred_element_type=F32)

    row = lax.broadcasted_iota(jnp.int32, (tq, tk), 0)
    col = lax.broadcasted_iota(jnp.int32, (tq, tk), 1)
    for h in range(n_heads):
        acc_ref[h] = jnp.zeros((tq, V7X_LANES), F32)
        r_ref[h] = jnp.zeros((tq, V7X_LANES), F32)
        lw_ref[h] = jnp.full((tq, tk), -ATT_MASKED, F32)
        stage_a(h, qi, col < row)

    def body(i, carry):
        jc = jnp.minimum(qi + 1 - i, qi)
        for h in range(n_heads):
            stage_c(h, jc)
        for h in range(n_heads):
            stage_b(h)
        for h in range(n_heads):
            stage_a(h, qi - 1 - i)
        return carry

    lax.fori_loop(0, qi, body, 0)
    for h in range(n_heads):
        stage_c(h, jnp.minimum(1, qi))
    for h in range(n_heads):
        stage_b(h)
    for h in range(n_heads):
        stage_c(h, 0)
    for pair in range(n_heads // HEADS_PER_LANE_GROUP):
        o = acc_ref[pair * HEADS_PER_LANE_GROUP]
        for hh in range(1, HEADS_PER_LANE_GROUP):
            o = jnp.where(lane >= hh * HEAD_DIM, acc_ref[pair * HEADS_PER_LANE_GROUP + hh], o)
        o_ref[0, :, pair * V7X_LANES:(pair + 1) * V7X_LANES] = o.astype(BF16)


def _sb_prompt(nbias, qb, ktb, vtb):
    b, s, sbw = qb.shape
    assert ATT_Q_ROWS == ATT_K_ROWS and s % ATT_Q_ROWS == 0
    tq = ATT_Q_ROWS
    n_heads = ATT_HEADS_PER_STEP
    gw = n_heads * HEAD_DIM
    assert n_heads % HEADS_PER_LANE_GROUP == 0 and sbw % gw == 0
    tri = (jnp.arange(ATT_K_ROWS)[:, None] > jnp.arange(ATT_K_ROWS)[None, :]).astype(BF16)
    return pl.pallas_call(
        _sb_prompt_kernel,
        grid=(b, sbw // gw, s // tq),
        in_specs=[pl.BlockSpec(memory_space=pltpu.SMEM),
                  pl.BlockSpec((1, tq, gw), lambda i, p, j: (i, j, p)),
                  pl.BlockSpec((1, gw, s), lambda i, p, j: (i, p, 0)),
                  pl.BlockSpec((1, gw, s), lambda i, p, j: (i, p, 0)),
                  _const_spec(tri.shape)],
        out_specs=pl.BlockSpec((1, tq, gw), lambda i, p, j: (i, j, p)),
        out_shape=jax.ShapeDtypeStruct((b, s, sbw), BF16),
        scratch_shapes=[pltpu.VMEM((n_heads, tq, V7X_LANES), F32), pltpu.VMEM((n_heads, tq, V7X_LANES), F32),
                        pltpu.VMEM((n_heads, tq, ATT_K_ROWS), F32), pltpu.VMEM((n_heads, tq, ATT_K_ROWS), F32)],
        compiler_params=_params(("parallel", "parallel", "arbitrary")),
        name="sb_prompt",
    )(nbias, qb, ktb, vtb, tri)


def _sb_decode_kernel(pt_ref, q_ref, nb_ref, tri_ref, *refs):
    del pt_ref
    g_pages = DEC_PAGES_PER_STEP
    k_refs, v_refs = refs[:g_pages], refs[g_pages:2 * g_pages]
    o_ref, acc_ref, r_ref = refs[2 * g_pages:]
    c = pl.program_id(1)
    heads = q_ref.shape[1]

    @pl.when(c == 0)
    def _():
        acc_ref[...] = jnp.zeros(acc_ref.shape, F32)
        r_ref[...] = jnp.zeros(r_ref.shape, F32)

    q = q_ref[0]
    zn = jnp.concatenate([jnp.sum(k_refs[g][0] * q, axis=1) for g in range(g_pages)], axis=0) + nb_ref[...]
    lk = _log_keep(zn)
    within = jnp.dot(lk.astype(BF16), tri_ref[...], preferred_element_type=F32)
    tot = jnp.sum(lk, axis=1, keepdims=True)
    carry = r_ref[...]
    carries = [None] * g_pages
    for g in reversed(range(g_pages)):
        carries[g] = carry
        carry = carry + tot[g * heads:(g + 1) * heads]
    r_ref[...] = carry
    w = jnp.exp((lk - zn) + (within + jnp.concatenate(carries, axis=0)))
    acc = acc_ref[...]
    for g in range(g_pages):
        acc = acc + v_refs[g][0] * w[g * heads:(g + 1) * heads][:, None, :]
    acc_ref[...] = acc

    @pl.when(c == pl.num_programs(1) - 1)
    def _():
        o_ref[0] = jnp.sum(acc, axis=2)


def _sb_decode(page_table, qrep, nbias, cache_kt, cache_vt):
    n, heads = qrep.shape[:2]
    n_pages = page_table.shape[1]
    g_pages = DEC_PAGES_PER_STEP
    assert n_pages % g_pages == 0 and cache_kt.shape[1:] == (heads, HEAD_DIM, PAGE)
    n_steps = n_pages // g_pages
    nb_tile = jnp.tile(jnp.broadcast_to(nbias[:, None], (heads, PAGE)), (g_pages, 1))
    tri = (jnp.arange(PAGE)[:, None] > jnp.arange(PAGE)[None, :]).astype(BF16)

    def page_spec(g):
        return pl.BlockSpec((1, heads, HEAD_DIM, PAGE),
                            lambda i, c, pt: (pt[i, (n_steps - 1 - c) * g_pages + g], 0, 0, 0))

    grid_spec = pltpu.PrefetchScalarGridSpec(
        num_scalar_prefetch=1,
        grid=(n, n_steps),
        in_specs=[pl.BlockSpec((1, heads, HEAD_DIM, PAGE), lambda i, c, pt: (i, 0, 0, 0)),
                  pl.BlockSpec(nb_tile.shape, lambda i, c, pt: (0, 0)),
                  pl.BlockSpec(tri.shape, lambda i, c, pt: (0, 0))]
                 + [page_spec(g) for g in range(g_pages)] + [page_spec(g) for g in range(g_pages)],
        out_specs=pl.BlockSpec((1, heads, HEAD_DIM), lambda i, c, pt: (i, 0, 0)),
        scratch_shapes=[pltpu.VMEM((heads, HEAD_DIM, PAGE), F32), pltpu.VMEM((heads, PAGE), F32)],
    )
    return pl.pallas_call(
        _sb_decode_kernel,
        grid_spec=grid_spec,
        out_shape=jax.ShapeDtypeStruct((n, heads, HEAD_DIM), F32),
        compiler_params=_params(("parallel", "arbitrary")),
        name="sb_decode",
    )(page_table, qrep, nb_tile, tri, *([cache_kt] * g_pages), *([cache_vt] * g_pages))


def _ab_out_ffn_kernel(x_ref, po_ref, att_ref, wout_ref, gffn_ref, wg_ref, wu_ref, wd_ref, y_ref):
    pw = po_ref.shape[1]
    y = (x_ref[...]
         + jnp.dot(po_ref[...], wout_ref[0:pw, :], preferred_element_type=F32)
         + jnp.dot(att_ref[...], wout_ref[pw:, :], preferred_element_type=F32))
    y_ref[...] = _ffn_residual(y, gffn_ref[...], wg_ref, wu_ref, wd_ref)


def _ab_out_ffn(x, po, att, w_out, gffn, wg, wu, wd):
    m, d = x.shape
    tm = min(FFN_ROWS, m)
    tile = lambda width: pl.BlockSpec((tm, width), lambda i: (i, 0))
    return pl.pallas_call(
        _ab_out_ffn_kernel,
        grid=(m // tm,),
        in_specs=[tile(d), tile(po.shape[1]), tile(att.shape[1]), _const_spec(w_out.shape),
                  _const_spec((1, d)), _const_spec(wg.shape), _const_spec(wu.shape), _const_spec(wd.shape)],
        out_specs=tile(d),
        out_shape=jax.ShapeDtypeStruct((m, d), F32),
        compiler_params=_params(("parallel",)),
        name="ab_out_ffn",
    )(x, po, att, w_out, gffn, wg, wu, wd)


def _cd_split(proj, gmw, scw):
    uv = jax.nn.gelu(proj[:, :2 * gmw])
    return (uv[:, :gmw], uv[:, gmw:], proj[:, 2 * gmw:2 * gmw + scw],
            proj[:, 2 * gmw + scw:2 * gmw + 2 * scw], proj[:, 2 * gmw + 2 * scw:])


def _cd_prompt_kernel(x_ref, gmix_ref, win_ref, ws_ref, bs_ref, cw_ref, wout_ref, gffn_ref,
                      wg_ref, wu_ref, wd_ref, gfin_ref, y_ref, ztail_ref, zext_ref):
    s = pl.program_id(1)
    tm = x_ref.shape[1]
    n_groups = ws_ref.shape[0]
    gmw = n_groups * ws_ref.shape[2]
    scw = cw_ref.shape[1]
    x = x_ref[0]
    h = _rmsnorm(x, gmix_ref[...]).astype(BF16)
    proj = jnp.dot(h, win_ref[...], preferred_element_type=F32)
    u, v, hh, bg, cg = _cd_split(proj, gmw, scw)

    r = lax.broadcasted_iota(jnp.int32, (CHUNK, CHUNK), 0)
    c = lax.broadcasted_iota(jnp.int32, (CHUNK, CHUNK), 1)
    vb = v.astype(BF16)
    gwid = gmw // n_groups
    wms = [jnp.where(c <= r, ws_ref[g], 0.0).astype(BF16) for g in range(n_groups)]
    rows = []
    for ch in range(tm // CHUNK):
        cols = [jnp.dot(wms[g], vb[ch * CHUNK:(ch + 1) * CHUNK, g * gwid:(g + 1) * gwid],
                        preferred_element_type=F32) for g in range(n_groups)]
        rows.append(jnp.concatenate(cols, axis=1) + bs_ref[...])
    gm = u * jnp.concatenate(rows, axis=0)

    z = cg * hh

    @pl.when(s == 0)
    def _():
        zext_ref[0:CONV_CARRY, :] = jnp.zeros((CONV_CARRY, scw), F32)

    zext_ref[CONV_CARRY:, :] = z
    zext = zext_ref[...]
    conv = z * cw_ref[CONV_W - 1:CONV_W, :]
    for j in range(CONV_W - 1):
        back = CONV_W - 1 - j
        conv = conv + pltpu.roll(zext, back, 0)[CONV_CARRY:, :] * cw_ref[j:j + 1, :]
    sc = bg * conv
    ztail_ref[0] = z[tm - CONV_CARRY:, :]
    zext_ref[0:CONV_CARRY, :] = z[tm - CONV_CARRY:, :]

    y = (x + jnp.dot(gm.astype(BF16), wout_ref[0:gmw, :], preferred_element_type=F32)
         + jnp.dot(sc.astype(BF16), wout_ref[gmw:, :], preferred_element_type=F32))
    y = _ffn_residual(y, gffn_ref[...], wg_ref, wu_ref, wd_ref)
    y_ref[0] = _rmsnorm(y, gfin_ref[...])


def _cd_prompt(x, gmix, w_in, w_s, bs_exp, conv_w, w_out, gffn, wg, wu, wd, gfin):
    b, s, d = x.shape
    scw = conv_w.shape[1]
    tm = CD_ROWS
    assert tm % CHUNK == 0 and s % tm == 0
    return pl.pallas_call(
        _cd_prompt_kernel,
        grid=(b, s // tm),
        in_specs=[pl.BlockSpec((1, tm, d), lambda i, j: (i, j, 0)), _const_spec((1, d)),
                  _const_spec(w_in.shape), _const_spec(w_s.shape), _const_spec(bs_exp.shape),
                  _const_spec(conv_w.shape), _const_spec(w_out.shape), _const_spec((1, d)),
                  _const_spec(wg.shape), _const_spec(wu.shape), _const_spec(wd.shape), _const_spec((1, d))],
        out_specs=[pl.BlockSpec((1, tm, d), lambda i, j: (i, j, 0)),
                   pl.BlockSpec((1, CONV_CARRY, scw), lambda i, j: (i, 0, 0))],
        out_shape=[jax.ShapeDtypeStruct((b, s, d), F32), jax.ShapeDtypeStruct((b, CONV_CARRY, scw), F32)],
        scratch_shapes=[pltpu.VMEM((tm + CONV_CARRY, scw), F32)],
        compiler_params=_params(("parallel", "arbitrary")),
        name="cd_prompt",
    )(x, gmix, w_in, w_s, bs_exp, conv_w, w_out, gffn, wg, wu, wd, gfin)


def _cd_sample_kernel(x_ref, gmix_ref, win_ref, ws_ref, bs_ref, cw_ref, cst_ref, wout_ref, gffn_ref,
                      wg_ref, wu_ref, wd_ref, gfin_ref, y_ref, z_ref, v_ref):
    n_groups = ws_ref.shape[0]
    gwid = ws_ref.shape[2]
    gmw = n_groups * gwid
    scw = cw_ref.shape[1]
    x = x_ref[...]
    h = _rmsnorm(x, gmix_ref[...]).astype(BF16)
    proj = jnp.dot(h, win_ref[...], preferred_element_type=F32)
    u, v, hh, bg, cg = _cd_split(proj, gmw, scw)
    v_ref[...] = v
    vb = v.astype(BF16).astype(F32)
    mixed = [vb[:, g * gwid:(g + 1) * gwid] * ws_ref[g, 0:1, 0:1].astype(BF16).astype(F32)
             for g in range(n_groups)]
    gm = u * (jnp.concatenate(mixed, axis=1) + bs_ref[0:1, :])
    z = cg * hh
    z_ref[...] = z
    conv = z * cw_ref[CONV_W - 1:CONV_W, :]
    for j in range(CONV_W - 1):
        conv = conv + cst_ref[j] * cw_ref[j:j + 1, :]
    sc = bg * conv
    y = (x + jnp.dot(gm.astype(BF16), wout_ref[0:gmw, :], preferred_element_type=F32)
         + jnp.dot(sc.astype(BF16), wout_ref[gmw:, :], preferred_element_type=F32))
    y = _ffn_residual(y, gffn_ref[...], wg_ref, wu_ref, wd_ref)
    y_ref[...] = _rmsnorm(y, gfin_ref[...])


def _cd_sample(x, gmix, w_in, w_s, bs_exp, conv_w, conv_state_t, w_out, gffn, wg, wu, wd, gfin):
    n, d = x.shape
    scw = conv_w.shape[1]
    gmw = w_s.shape[0] * w_s.shape[2]
    ins = (x, gmix, w_in, w_s, bs_exp, conv_w, conv_state_t, w_out, gffn, wg, wu, wd, gfin)
    full = lambda a: _const_spec(a.shape)
    out = lambda width: (pl.BlockSpec((n, width), lambda i: (0, 0)), jax.ShapeDtypeStruct((n, width), F32))
    outs = [out(d), out(scw), out(gmw)]
    return pl.pallas_call(
        _cd_sample_kernel,
        grid=(1,),
        in_specs=[full(a) for a in ins],
        out_specs=[o[0] for o in outs],
        out_shape=[o[1] for o in outs],
        compiler_params=_params(("arbitrary",)),
        name="cd_sample",
    )(*ins)


def kernel(x_prompt, x_sample, cache_k, cache_v, state_pool, state_conv, page_table, norm_mix, norm_ffn,
           norm_final, ab_w_in, ab_sb_bias, ab_w_pool, ab_pool_scale, ab_w_out, cd_w_in, cd_w_s, cd_b_s,
           cd_conv_w, cd_w_out, ffn_w_gate, ffn_w_up, ffn_w_down):
    b, s, d = x_prompt.shape
    n = x_sample.shape[0]
    heads = ab_sb_bias.shape[1]
    assert x_sample.shape[1] == 1 and norm_mix.shape[0] == 2
    bf = lambda a: a.astype(BF16)
    row = lambda a: a.reshape(1, -1)
    xs = x_sample.reshape(n, d)

    w_in, w_pool, w_out = bf(ab_w_in[0]), bf(ab_w_pool[0]), bf(ab_w_out[0])
    wg, wu, wd = bf(ffn_w_gate[0]), bf(ffn_w_up[0]), bf(ffn_w_down[0])
    gmix, gffn, pscale = row(norm_mix[0]), row(norm_ffn[0]), row(ab_pool_scale[0])
    nbias = -ab_sb_bias[0]
    sbw = heads * HEAD_DIM
    n_uq = w_in.shape[1] - 2 * sbw

    ktp, vtp, qb, ktb, vtb, po, u_tail = _ab_in_prompt(x_prompt, gmix, w_in[:, :n_uq], w_in[:, n_uq:].T,
                                                       w_pool, pscale)
    att = _sb_prompt(nbias, qb, ktb, vtb)
    yp = _ab_out_ffn(x_prompt.reshape(b * s, d), po.reshape(b * s, -1), att.reshape(b * s, -1),
                     w_out, gffn, wg, wu, wd).reshape(b, s, d)

    state_t = jnp.swapaxes(state_pool[0], 0, 1)
    us, qs, ks, vs, pos = _ab_in_sample(xs, gmix, w_in, w_pool, pscale, state_t)
    qrep = jnp.broadcast_to(qs.reshape(n, heads, HEAD_DIM, 1), (n, heads, HEAD_DIM, PAGE))
    cache_kt = jnp.transpose(cache_k[0], (0, 2, 3, 1))
    cache_vt = jnp.transpose(cache_v[0], (0, 2, 3, 1))
    att_s = _sb_decode(page_table, qrep, nbias, cache_kt, cache_vt)
    ys = _ab_out_ffn(xs, pos, bf(att_s.reshape(n, sbw)), w_out, gffn, wg, wu, wd)

    k_prompt = jnp.transpose(ktp.reshape(b, heads, HEAD_DIM, s), (0, 3, 1, 2))[None]
    v_prompt = jnp.transpose(vtp.reshape(b, heads, HEAD_DIM, s), (0, 3, 1, 2))[None]
    k_sample = ks.reshape(1, n, 1, heads, HEAD_DIM)
    v_sample = vs.reshape(1, n, 1, heads, HEAD_DIM)
    pool_prompt = u_tail[None, :, POOL_CARRY - POOL_BUF:, :]
    pool_sample = jnp.concatenate([state_pool[0][:, 1:, :], us[:, None, :]], axis=1)[None]

    w_in, w_out = bf(cd_w_in[0]), bf(cd_w_out[0])
    wg, wu, wd = bf(ffn_w_gate[1]), bf(ffn_w_up[1]), bf(ffn_w_down[1])
    gmix, gffn, gfin = row(norm_mix[1]), row(norm_ffn[1]), row(norm_final)
    gwid = cd_w_s.shape[3]
    bs_exp = jnp.repeat(cd_b_s[0].T, gwid, axis=1)
    conv_w = cd_conv_w[0]

    yp, z_tail = _cd_prompt(yp, gmix, w_in, cd_w_s[0], bs_exp, conv_w, w_out, gffn, wg, wu, wd, gfin)
    conv_state_t = jnp.swapaxes(state_conv[0], 0, 1)
    ys, zs, chv = _cd_sample(ys, gmix, w_in, cd_w_s[0], bs_exp, conv_w, conv_state_t, w_out, gffn,
                             wg, wu, wd, gfin)

    conv_prompt = z_tail[None, :, CONV_CARRY - (CONV_W - 1):, :]
    conv_sample = jnp.concatenate([state_conv[0][:, 1:, :], zs[:, None, :]], axis=1)[None]
    return (yp, ys.reshape(n, 1, d), k_prompt, v_prompt, k_sample, v_sample, pool_prompt, pool_sample,
            conv_prompt, conv_sample, chv.reshape(1, n, 1, -1))
```
